```python
import math
import jax, jax.numpy as jnp
from jax import lax
import numpy as np

D_MODEL = 1024
BATCH = 16
SEQ = 4096
DEPTH = 1

MEM_LEN = 256
EXPAND = 2
D_MIX = EXPAND * D_MODEL
D_CONV = 3 * D_MIX // 8
D_GMLP = 3 * D_MIX // 8
D_XATT = D_MIX - D_CONV - D_GMLP
N_XHEADS = 4
XHEAD_DIM = D_XATT // N_XHEADS
CONV_WIDTH = 31
N_CONV_GROUPS = D_CONV // 128
CHUNK = 128
GMLP_HEAD = 128
N_GMLP_HEADS = D_GMLP // GMLP_HEAD
ALPHA = (2.0 * DEPTH) ** 0.25
BETA = (8.0 * DEPTH) ** -0.25
LN_EPS = 1e-5
D_IN_PROJ = 3 * D_CONV + 3 * D_GMLP + 2 * D_XATT

kernel_name = "hybrid_conv_gmlp_memxattn_deepnorm"


def _layernorm(h, g, b):
    hf = h.astype(jnp.float32)
    mu = jnp.mean(hf, axis=-1, keepdims=True)
    var = jnp.mean(jnp.square(hf - mu), axis=-1, keepdims=True)
    return ((hf - mu) * lax.rsqrt(var + LN_EPS)).astype(h.dtype) * g + b


def _conv_branch(a, glu_g, conv_w, conv_b, ln_g, ln_b):
    h = a * jax.nn.sigmoid(glu_g)
    h = lax.conv_general_dilated(
        h, conv_w[:, None, :].astype(h.dtype), window_strides=(1,),
        padding=[(CONV_WIDTH - 1, 0)],
        dimension_numbers=("NWC", "WIO", "NWC"),
        feature_group_count=D_CONV) + conv_b
    h = _layernorm(h, ln_g, ln_b)
    return jax.nn.silu(h)


def _gmlp_branch(u, v, ln_g, ln_b, ws, bs):
    B, S, _ = u.shape
    v = _layernorm(v, ln_g, ln_b)
    v = v.reshape(B, S // CHUNK, CHUNK, N_GMLP_HEADS, GMLP_HEAD)
    causal = jnp.tril(jnp.ones((CHUNK, CHUNK), dtype=bool))
    ws_c = jnp.where(causal[None], ws, jnp.zeros((), ws.dtype))
    s = jnp.einsum("hts,bnshc->bnthc", ws_c, v) + jnp.transpose(bs)[None, None, :, :, None]
    return u * s.reshape(B, S, D_GMLP)


def _mem_xattn(q, mem, w_kv, b_kv):
    B, S, _ = q.shape
    kv = jnp.einsum("bmd,de->bme", mem, w_kv) + b_kv
    k, v = jnp.split(kv, 2, axis=-1)
    q = q.reshape(B, S, N_XHEADS, XHEAD_DIM)
    k = k.reshape(B, MEM_LEN, N_XHEADS, XHEAD_DIM)
    v = v.reshape(B, MEM_LEN, N_XHEADS, XHEAD_DIM)
    scores = jnp.einsum("bshd,bmhd->bhsm", q, k).astype(jnp.float32) * (XHEAD_DIM ** -0.5)
    p = jax.nn.softmax(scores, axis=-1).astype(v.dtype)
    o = jnp.einsum("bhsm,bmhd->bshd", p, v)
    return o.reshape(B, S, D_XATT)


def setup_inputs(seed: int = 0) -> dict:
    key = jax.random.key(seed)
    ks = jax.random.split(key, 20)
    f32 = jnp.float32
    L = DEPTH
    nrm = lambda k, shp: jax.random.normal(k, shp, f32)
    return {
        "x": nrm(ks[0], (BATCH, SEQ, D_MODEL)),
        "mem": nrm(ks[1], (BATCH, MEM_LEN, D_MODEL)),
        "w_in": nrm(ks[2], (L, D_MODEL, D_IN_PROJ)) * D_MODEL ** -0.5,
        "b_in": 0.02 * nrm(ks[3], (L, D_IN_PROJ)),
        "conv_w": nrm(ks[4], (L, CONV_WIDTH, D_CONV)) * CONV_WIDTH ** -0.5,
        "conv_b": 0.02 * nrm(ks[5], (L, D_CONV)),
        "conv_ln_g": 1.0 + 0.05 * nrm(ks[6], (L, D_CONV)),
        "conv_ln_b": 0.02 * nrm(ks[7], (L, D_CONV)),
        "gmlp_ln_g": 1.0 + 0.05 * nrm(ks[8], (L, D_GMLP)),
        "gmlp_ln_b": 0.02 * nrm(ks[9], (L, D_GMLP)),
        "gmlp_ws": nrm(ks[10], (L, N_GMLP_HEADS, CHUNK, CHUNK)) * CHUNK ** -0.5,
        "gmlp_bs": 1.0 + 0.05 * nrm(ks[11], (L, N_GMLP_HEADS, CHUNK)),
        "w_kv": nrm(ks[12], (L, D_MODEL, 2 * D_XATT)) * D_MODEL ** -0.5,
        "b_kv": 0.02 * nrm(ks[13], (L, 2 * D_XATT)),
        "w_out": nrm(ks[14], (L, D_MIX, D_MODEL)) * (D_MIX ** -0.5) * BETA,
        "b_out": 0.02 * nrm(ks[15], (L, D_MODEL)),
        "ln_g": 1.0 + 0.05 * nrm(ks[16], (L, D_MODEL)),
        "ln_b": 0.02 * nrm(ks[17], (L, D_MODEL)),
    }


def reference(x, mem, w_in, b_in, conv_w, conv_b, conv_ln_g, conv_ln_b, gmlp_ln_g, gmlp_ln_b,
              gmlp_ws, gmlp_bs, w_kv, b_kv, w_out, b_out, ln_g, ln_b):
    split_pts = np.cumsum([D_CONV, D_CONV, D_CONV, D_GMLP, D_GMLP, D_GMLP, D_XATT])
    for l in range(DEPTH):
        h = jnp.einsum("bsd,de->bse", x, w_in[l]) + b_in[l]
        (c_a, c_glu, c_gate, g_u, g_v, g_gate, xq, x_gate) = jnp.split(
            h, [int(p) for p in split_pts], axis=-1)
        y_conv = _conv_branch(c_a, c_glu, conv_w[l], conv_b[l], conv_ln_g[l], conv_ln_b[l])
        y_gmlp = _gmlp_branch(jax.nn.gelu(g_u), jax.nn.gelu(g_v), gmlp_ln_g[l], gmlp_ln_b[l],
                              gmlp_ws[l], gmlp_bs[l])
        y_xatt = _mem_xattn(xq, mem, w_kv[l], b_kv[l])
        y = jnp.concatenate([y_conv * jax.nn.silu(c_gate),
                             y_gmlp * jax.nn.silu(g_gate),
                             y_xatt * jax.nn.silu(x_gate)], axis=-1)
        y = jnp.einsum("bse,ed->bsd", y, w_out[l]) + b_out[l]
        x = _layernorm(ALPHA * x + y, ln_g[l], ln_b[l])
    return x
```

```python
import functools
import math

import jax
import jax.numpy as jnp
from jax import lax
from jax.experimental import pallas as pl
from jax.experimental.pallas import tpu as pltpu

F32 = jnp.float32
BF16 = jnp.bfloat16

D_MODEL = 1024
MEM_LEN = 256
D_MIX = 2 * D_MODEL
D_CONV = 3 * D_MIX // 8
D_GMLP = 3 * D_MIX // 8
D_XATT = D_MIX - D_CONV - D_GMLP
N_XHEADS = 4
XHEAD_DIM = D_XATT // N_XHEADS
CONV_WIDTH = 31
CHUNK = 128
N_GMLP_HEADS = D_GMLP // CHUNK
LN_EPS = 1e-5
D_IN_PROJ = 3 * D_CONV + 3 * D_GMLP + 2 * D_XATT

LANES = 128
SUBLANES = 8
V7X_VMEM_BYTES = 64 * 1024 * 1024

SEQ_TILE = 512
CONV_HIST = 32
CONV_ROW_STRIDE = 4
CONV_BLOCK = SUBLANES * CONV_ROW_STRIDE
ROW_BLOCK = 16
ATT_ROWS = 128

N_CONV_T = D_CONV // LANES
N_GMLP_T = D_GMLP // LANES
N_XATT_T = D_XATT // LANES
N_MIX_T = D_MIX // LANES
COL_GMLP = 3 * D_CONV
COL_XATT = COL_GMLP + 3 * D_GMLP


def _sigmoid(x):
    return 0.5 + 0.5 * jnp.tanh(0.5 * x)


def _silu(x):
    hx = 0.5 * x
    return hx + hx * jnp.tanh(hx)


def _gelu_tanh(x):
    c = math.sqrt(2.0 / math.pi)
    hx = 0.5 * x
    return hx + hx * jnp.tanh(x * (c + (c * 0.044715) * (x * x)))


def _row_stats(tiles, width):
    tot = tiles[0]
    for t in tiles[1:]:
        tot = tot + t
    mu = jnp.sum(tot, axis=-1, keepdims=True) * (1.0 / width)
    cen = [t - mu for t in tiles]
    sq = cen[0] * cen[0]
    for c in cen[1:]:
        sq = sq + c * c
    var = jnp.sum(sq, axis=-1, keepdims=True) * (1.0 / width)
    return cen, lax.rsqrt(var + LN_EPS)


def _kv_kernel(mem_ref, w_ref, b_ref, kt_ref, v_ref):
    kv = jnp.dot(mem_ref[0].astype(BF16), w_ref[...], preferred_element_type=F32) + b_ref[...]
    kt_ref[0] = kv[:, :D_XATT].T.astype(BF16)
    v_ref[0] = kv[:, D_XATT:].astype(BF16)


def _layer_kernel(x_ref, kt_ref, v_ref, w_in_ref, b_in_ref, cw_ref, cb_ref, clg_ref, clb_ref,
                  glg_ref, glb_ref, ws_ref, bs_ref, w_out_ref, b_out_ref, lg_ref, lb_ref,
                  o_ref, xb_ref, h_ref, cbuf_ref, vn_ref, yf_ref, *, alpha):
    ts = SEQ_TILE
    j = pl.program_id(1)

    def lane_tile(ref, n):
        return ref[0:1, n * LANES:(n + 1) * LANES]

    def project(col0, ntiles):
        res = jnp.dot(xb_ref[...], w_in_ref[:, col0:col0 + ntiles * LANES],
                      preferred_element_type=F32)
        for n in range(ntiles):
            c0 = col0 + n * LANES
            h_ref[n] = res[:, n * LANES:(n + 1) * LANES] + b_in_ref[0:1, c0:c0 + LANES]

    xb_ref[...] = x_ref[0].astype(BF16)

    @pl.when(j == 0)
    def _():
        cbuf_ref[:, 0:CONV_HIST, :] = jnp.zeros((N_CONV_T, CONV_HIST, LANES), F32)

    @pl.when(j > 0)
    def _():
        cbuf_ref[:, 0:CONV_HIST, :] = cbuf_ref[:, ts:ts + CONV_HIST, :]

    project(0, 3 * N_CONV_T)

    def glu_body(i, carry):
        r = pl.multiple_of(i * CONV_BLOCK, CONV_BLOCK)
        for cg in range(N_CONV_T):
            a = h_ref[cg, pl.ds(r, CONV_BLOCK), :]
            g = h_ref[N_CONV_T + cg, pl.ds(r, CONV_BLOCK), :]
            cbuf_ref[cg, pl.ds(CONV_HIST + r, CONV_BLOCK), :] = a * _sigmoid(g)
        return carry

    lax.fori_loop(0, ts // CONV_BLOCK, glu_body, 0)

    def conv_body(i, carry):
        r = pl.multiple_of(i * CONV_BLOCK, CONV_BLOCK)
        first = r + CONV_HIST - (CONV_WIDTH - 1)
        accs = []
        for cg in range(N_CONV_T):
            acc = [jnp.zeros((SUBLANES, LANES), F32) for _ in range(CONV_ROW_STRIDE)]
            for k in range(CONV_WIDTH):
                w = cw_ref[k:k + 1, cg * LANES:(cg + 1) * LANES]
                for p in range(CONV_ROW_STRIDE):
                    rows = pl.ds(first + p + k, SUBLANES, stride=CONV_ROW_STRIDE)
                    acc[p] = acc[p] + w * cbuf_ref[cg, rows, :]
            accs.append([a + lane_tile(cb_ref, cg) for a in acc])
        for p in range(CONV_ROW_STRIDE):
            rows = pl.ds(r + p, SUBLANES, stride=CONV_ROW_STRIDE)
            cen, inv = _row_stats([accs[cg][p] for cg in range(N_CONV_T)], D_CONV)
            for cg in range(N_CONV_T):
                hn = cen[cg] * inv * lane_tile(clg_ref, cg) + lane_tile(clb_ref, cg)
                gate = h_ref[2 * N_CONV_T + cg, rows, :]
                yf_ref[cg, rows, :] = _silu(hn) * _silu(gate)
        return carry

    lax.fori_loop(0, ts // CONV_BLOCK, conv_body, 0)

    project(COL_GMLP, 3 * N_GMLP_T)

    def gmlp_body(i, carry):
        r = pl.multiple_of(i * ROW_BLOCK, ROW_BLOCK)
        rows = pl.ds(r, ROW_BLOCK)
        vs = [_gelu_tanh(h_ref[N_GMLP_T + hd, rows, :]) for hd in range(N_GMLP_HEADS)]
        cen, inv = _row_stats(vs, D_GMLP)
        for hd in range(N_GMLP_HEADS):
            vn = cen[hd] * inv * lane_tile(glg_ref, hd) + lane_tile(glb_ref, hd)
            vn_ref[hd, rows, :] = vn.astype(BF16)
            u = _gelu_tanh(h_ref[hd, rows, :])
            h_ref[hd, rows, :] = u * _silu(h_ref[2 * N_GMLP_T + hd, rows, :])
        return carry

    lax.fori_loop(0, ts // ROW_BLOCK, gmlp_body, 0)

    tril = (lax.broadcasted_iota(jnp.int32, (CHUNK, CHUNK), 0)
            >= lax.broadcasted_iota(jnp.int32, (CHUNK, CHUNK), 1))
    for hd in range(N_GMLP_HEADS):
        ws_c = jnp.where(tril, ws_ref[hd], 0.0).astype(BF16)
        for c in range(ts // CHUNK):
            rows = pl.ds(c * CHUNK, CHUNK)
            s = jnp.dot(ws_c, vn_ref[hd, rows, :], preferred_element_type=F32) + bs_ref[hd]
            yf_ref[N_CONV_T + hd, rows, :] = h_ref[hd, rows, :] * s

    project(COL_XATT, 2 * N_XATT_T)
    scale = XHEAD_DIM ** -0.5
    for hd in range(N_XHEADS):
        kt = kt_ref[0, hd * XHEAD_DIM:(hd + 1) * XHEAD_DIM, :]
        vh = v_ref[0, :, hd * XHEAD_DIM:(hd + 1) * XHEAD_DIM]
        for c in range(ts // ATT_ROWS):
            rows = pl.ds(c * ATT_ROWS, ATT_ROWS)
            q = h_ref[hd, rows, :].astype(BF16)
            s = jnp.dot(q, kt, preferred_element_type=F32) * scale
            e = jnp.exp(s - jnp.max(s, axis=-1, keepdims=True))
            l = jnp.sum(e, axis=-1, keepdims=True)
            o = jnp.dot(e.astype(BF16), vh, preferred_element_type=F32) * (1.0 / l)
            yf_ref[2 * N_CONV_T + hd, rows, :] = o * _silu(h_ref[N_XATT_T + hd, rows, :])

    y = jnp.concatenate([yf_ref[n].astype(BF16) for n in range(N_MIX_T)], axis=1)
    z = jnp.dot(y, w_out_ref[...], preferred_element_type=F32)
    o_ref[0] = z + b_out_ref[...] + alpha * x_ref[0]

    def ln_body(i, carry):
        r = pl.multiple_of(i * SUBLANES, SUBLANES)
        rows = pl.ds(r, SUBLANES)
        zz = o_ref[0, rows, :]
        mu = jnp.mean(zz, axis=-1, keepdims=True)
        cen = zz - mu
        var = jnp.mean(cen * cen, axis=-1, keepdims=True)
        o_ref[0, rows, :] = cen * lax.rsqrt(var + LN_EPS) * lg_ref[...] + lb_ref[...]
        return carry

    lax.fori_loop(0, ts // SUBLANES, ln_body, 0)


def _const_spec(shape):
    return pl.BlockSpec(shape, lambda *_: (0,) * len(shape), pipeline_mode=pl.Buffered(1))


def _layer(x, mem, w_in, b_in, conv_w, conv_b, conv_ln_g, conv_ln_b, gmlp_ln_g, gmlp_ln_b,
           gmlp_ws, gmlp_bs, w_kv, b_kv, w_out, b_out, ln_g, ln_b, alpha):
    batch, seq, d = x.shape
    assert d == D_MODEL and seq % SEQ_TILE == 0 and SEQ_TILE % CHUNK == 0
    assert mem.shape == (batch, MEM_LEN, D_MODEL)
    row = lambda a: a.reshape(1, -1)

    kt, v = pl.pallas_call(
        _kv_kernel,
        grid=(batch,),
        in_specs=[pl.BlockSpec((1, MEM_LEN, D_MODEL), lambda b: (b, 0, 0)),
                  _const_spec((D_MODEL, 2 * D_XATT)),
                  _const_spec((1, 2 * D_XATT))],
        out_specs=[pl.BlockSpec((1, D_XATT, MEM_LEN), lambda b: (b, 0, 0)),
                   pl.BlockSpec((1, MEM_LEN, D_XATT), lambda b: (b, 0, 0))],
        out_shape=[jax.ShapeDtypeStruct((batch, D_XATT, MEM_LEN), BF16),
                   jax.ShapeDtypeStruct((batch, MEM_LEN, D_XATT), BF16)],
        name="mem_kv",
    )(mem, w_kv.astype(BF16), row(b_kv))

    bs_cols = jnp.broadcast_to(gmlp_bs[:, :, None], (N_GMLP_HEADS, CHUNK, LANES))

    return pl.pallas_call(
        functools.partial(_layer_kernel, alpha=alpha),
        grid=(batch, seq // SEQ_TILE),
        in_specs=[pl.BlockSpec((1, SEQ_TILE, D_MODEL), lambda b, j: (b, j, 0)),
                  pl.BlockSpec((1, D_XATT, MEM_LEN), lambda b, j: (b, 0, 0)),
                  pl.BlockSpec((1, MEM_LEN, D_XATT), lambda b, j: (b, 0, 0)),
                  _const_spec((D_MODEL, D_IN_PROJ)),
                  _const_spec((1, D_IN_PROJ)),
                  _const_spec((CONV_WIDTH, D_CONV)),
                  _const_spec((1, D_CONV)), _const_spec((1, D_CONV)), _const_spec((1, D_CONV)),
                  _const_spec((1, D_GMLP)), _const_spec((1, D_GMLP)),
                  _const_spec((N_GMLP_HEADS, CHUNK, CHUNK)),
                  _const_spec((N_GMLP_HEADS, CHUNK, LANES)),
                  _const_spec((D_MIX, D_MODEL)),
                  _const_spec((1, D_MODEL)), _const_spec((1, D_MODEL)), _const_spec((1, D_MODEL))],
        out_specs=pl.BlockSpec((1, SEQ_TILE, D_MODEL), lambda b, j: (b, j, 0)),
        out_shape=jax.ShapeDtypeStruct((batch, seq, D_MODEL), F32),
        scratch_shapes=[
            pltpu.VMEM((SEQ_TILE, D_MODEL), BF16),
            pltpu.VMEM((3 * N_CONV_T, SEQ_TILE, LANES), F32),
            pltpu.VMEM((N_CONV_T, SEQ_TILE + CONV_HIST, LANES), F32),
            pltpu.VMEM((N_GMLP_HEADS, SEQ_TILE, LANES), BF16),
            pltpu.VMEM((N_MIX_T, SEQ_TILE, LANES), F32),
        ],
        compiler_params=pltpu.CompilerParams(
            dimension_semantics=("arbitrary", "arbitrary"),
            vmem_limit_bytes=V7X_VMEM_BYTES - 8 * 1024 * 1024),
        name="hybrid_layer",
    )(x, kt, v, w_in.astype(BF16), row(b_in), conv_w, row(conv_b), row(conv_ln_g),
      row(conv_ln_b), row(gmlp_ln_g), row(gmlp_ln_b), gmlp_ws, bs_cols, w_out.astype(BF16),
      row(b_out), row(ln_g), row(ln_b))


def kernel(x, mem, w_in, b_in, conv_w, conv_b, conv_ln_g, conv_ln_b, gmlp_ln_g, gmlp_ln_b,
           gmlp_ws, gmlp_bs, w_kv, b_kv, w_out, b_out, ln_g, ln_b):
    depth = w_in.shape[0]
    alpha = (2.0 * depth) ** 0.25
    for l in range(depth):
        x = _layer(x, mem, w_in[l], b_in[l], conv_w[l], conv_b[l], conv_ln_g[l], conv_ln_b[l],
                   gmlp_ln_g[l], gmlp_ln_b[l], gmlp_ws[l], gmlp_bs[l], w_kv[l], b_kv[l],
                   w_out[l], b_out[l], ln_g[l], ln_b[l], alpha)
    return x
```

```python
import functools
import math

import jax
import jax.numpy as jnp
from jax import lax
from jax.experimental import pallas as pl
from jax.experimental.pallas import tpu as pltpu

F32 = jnp.float32
BF16 = jnp.bfloat16

D_MODEL = 1024
MEM_LEN = 256
D_MIX = 2 * D_MODEL
D_CONV = 3 * D_MIX // 8
D_GMLP = 3 * D_MIX // 8
D_XATT = D_MIX - D_CONV - D_GMLP
N_XHEADS = 4
XHEAD_DIM = D_XATT // N_XHEADS
CONV_WIDTH = 31
CHUNK = 128
N_GMLP_HEADS = D_GMLP // CHUNK
LN_EPS = 1e-5
D_IN_PROJ = 3 * D_CONV + 3 * D_GMLP + 2 * D_XATT

LANES = 128
SUBLANES = 8
PACKED_ROWS = 16
V7X_VMEM_BYTES = 64 * 1024 * 1024

SEQ_TILE = 512
SUB = CHUNK
N_SUB = SEQ_TILE // SUB
CONV_HIST = 32
CONV_ROW_STRIDE = 4
CONV_BLOCK = SUBLANES * CONV_ROW_STRIDE
PROJ_TILES = 2

N_CONV_T = D_CONV // LANES
N_GMLP_T = D_GMLP // LANES
N_XATT_T = D_XATT // LANES
N_IN_T = D_IN_PROJ // LANES
N_OUT_T = D_MODEL // LANES
T_CONV_A, T_CONV_GLU, T_CONV_GATE = 0, N_CONV_T, 2 * N_CONV_T
T_GMLP_U, T_GMLP_V, T_GMLP_GATE = 3 * N_CONV_T, 3 * N_CONV_T + N_GMLP_T, 3 * N_CONV_T + 2 * N_GMLP_T
T_XATT_Q = 3 * N_CONV_T + 3 * N_GMLP_T
T_XATT_GATE = T_XATT_Q + N_XATT_T
Y_CONV, Y_GMLP, Y_XATT = 0, N_CONV_T, N_CONV_T + N_GMLP_T

GELU_C = math.sqrt(2.0 / math.pi)


def _half_silu(hx):
    return hx + hx * jnp.tanh(hx)


def _half_gelu_tanh(hx):
    inner = hx * (2.0 * GELU_C + (8.0 * GELU_C * 0.044715) * (hx * hx))
    return hx + hx * jnp.tanh(inner)


def _row_stats(tiles, width):
    tot = tiles[0]
    for t in tiles[1:]:
        tot = tot + t
    mu = jnp.sum(tot, axis=-1, keepdims=True) * (1.0 / width)
    cen = [t - mu for t in tiles]
    sq = cen[0] * cen[0]
    for c in cen[1:]:
        sq = sq + c * c
    var = jnp.sum(sq, axis=-1, keepdims=True) * (1.0 / width)
    return cen, lax.rsqrt(var + LN_EPS)


def _kv_kernel(mem_ref, w_ref, b_ref, kt_ref, v_ref):
    kv = jnp.dot(mem_ref[0].astype(BF16), w_ref[...], preferred_element_type=F32) + b_ref[...]
    kt_ref[0] = kv[:, :D_XATT].T.astype(BF16)
    v_ref[0] = kv[:, D_XATT:].astype(BF16)


def _layer_kernel(x_ref, xn_ref, kt_ref, v_ref, w_in_ref, b_in_ref, cw_ref, cb_ref, clg_ref,
                  clb_ref, glg_ref, glb_ref, ws_ref, bs_ref, w_out_ref, b_out_ref, lg_ref, lb_ref,
                  o_ref, xb_ref, h_ref, cbuf_ref, cout_ref, vn_ref, ug_ref, y_ref,
                  wsm_ref, *, alpha):
    ts = SEQ_TILE
    first_step = jnp.logical_and(pl.program_id(0) == 0, pl.program_id(1) == 0)

    def lane_tile(ref, n):
        return ref[0:1, n * LANES:(n + 1) * LANES]

    def project_cols(row0, h_dst, n0, ntiles):
        res = jnp.dot(xb_ref[pl.ds(row0, SUB), :], w_in_ref[:, n0 * LANES:(n0 + ntiles) * LANES],
                      preferred_element_type=F32)
        for n in range(ntiles):
            h_dst[n0 + n] = res[:, n * LANES:(n + 1) * LANES] + lane_tile(b_in_ref, n0 + n)

    xb_ref[0:ts, :] = x_ref[0].astype(BF16)
    xb_ref[ts:ts + SUB, :] = xn_ref[0].astype(BF16)

    @pl.when(first_step)
    def _():
        project_cols(0, h_ref.at[0], 0, N_IN_T)

    @pl.when(pl.program_id(1) == 0)
    def _():
        cbuf_ref[:, 0:CONV_HIST, :] = jnp.zeros((N_CONV_T, CONV_HIST, LANES), F32)

    @pl.when(pl.program_id(1) > 0)
    def _():
        cbuf_ref[:, 0:CONV_HIST, :] = cbuf_ref[:, ts:ts + CONV_HIST, :]

    tril = (lax.broadcasted_iota(jnp.int32, (CHUNK, CHUNK), 0)
            >= lax.broadcasted_iota(jnp.int32, (CHUNK, CHUNK), 1))
    for hd in range(N_GMLP_HEADS):
        wsm_ref[hd] = jnp.where(tril, ws_ref[hd], 0.0).astype(BF16)
    att_scale = (XHEAD_DIM ** -0.5) * math.log2(math.e)

    def sub_block(s, h_cur, h_nxt):
        base = pl.multiple_of(s * SUB, SUB)
        vec = []

        def glu(cg):
            a = h_cur[T_CONV_A + cg]
            g = h_cur[T_CONV_GLU + cg]
            cbuf_ref[cg, pl.ds(CONV_HIST + base, SUB), :] = a + a * jnp.tanh(g)

        def conv(rb, cg):
            r = rb * CONV_BLOCK
            first = base + r + CONV_HIST - (CONV_WIDTH - 1)
            acc = [None] * CONV_ROW_STRIDE
            for k in range(CONV_WIDTH):
                w = cw_ref[k:k + 1, cg * LANES:(cg + 1) * LANES]
                for p in range(CONV_ROW_STRIDE):
                    rows = pl.ds(first + p + k, SUBLANES, stride=CONV_ROW_STRIDE)
                    term = w * cbuf_ref[cg, rows, :]
                    acc[p] = term if acc[p] is None else acc[p] + term
            for p in range(CONV_ROW_STRIDE):
                cout_ref[cg, pl.ds(r + p, SUBLANES, stride=CONV_ROW_STRIDE), :] = acc[p]

        def conv_norm(rb):
            rows = pl.ds(rb * PACKED_ROWS, PACKED_ROWS)
            cv = [cout_ref[cg, rows, :] + lane_tile(cb_ref, cg) for cg in range(N_CONV_T)]
            cen, inv = _row_stats(cv, D_CONV)
            for cg in range(N_CONV_T):
                hn = cen[cg] * inv * lane_tile(clg_ref, cg) + lane_tile(clb_ref, cg)
                gate = _half_silu(h_cur[T_CONV_GATE + cg, rows, :])
                y_ref[rows, (Y_CONV + cg) * LANES:(Y_CONV + cg + 1) * LANES] = (
                    _half_silu(0.5 * hn) * gate).astype(BF16)

        vec += [(12, functools.partial(glu, cg)) for cg in range(N_CONV_T)]
        for rb in range(SUB // CONV_BLOCK):
            vec += [(62, functools.partial(conv, rb, cg)) for cg in range(N_CONV_T)]
            vec += [(42, functools.partial(conv_norm, 2 * rb + i)) for i in range(2)]

        def gmlp_pre(rb):
            rows = pl.ds(rb * PACKED_ROWS, PACKED_ROWS)
            vs = [_half_gelu_tanh(h_cur[T_GMLP_V + hd, rows, :])
                  for hd in range(N_GMLP_HEADS)]
            cen, inv = _row_stats(vs, D_GMLP)
            for hd in range(N_GMLP_HEADS):
                vn = cen[hd] * inv * lane_tile(glg_ref, hd) + lane_tile(glb_ref, hd)
                vn_ref[hd, rows, :] = vn.astype(BF16)
                u = _half_gelu_tanh(h_cur[T_GMLP_U + hd, rows, :])
                ug_ref[hd, rows, :] = u * _half_silu(h_cur[T_GMLP_GATE + hd, rows, :])

        def gmlp_mix(hd):
            mix = jnp.dot(wsm_ref[hd], vn_ref[hd], preferred_element_type=F32) + bs_ref[hd]
            y_ref[:, (Y_GMLP + hd) * LANES:(Y_GMLP + hd + 1) * LANES] = (
                ug_ref[hd] * mix).astype(BF16)

        vec += [(70, functools.partial(gmlp_pre, rb)) for rb in range(SUB // PACKED_ROWS)]
        vec += [(12, functools.partial(gmlp_mix, hd)) for hd in range(N_GMLP_HEADS)]

        def attend(hd):
            kt = kt_ref[0, hd * XHEAD_DIM:(hd + 1) * XHEAD_DIM, :]
            vh = v_ref[0, :, hd * XHEAD_DIM:(hd + 1) * XHEAD_DIM]
            q = h_cur[T_XATT_Q + hd].astype(BF16)
            t = jnp.dot(q, kt, preferred_element_type=F32) * att_scale
            e = jnp.exp2(t - jnp.max(t, axis=-1, keepdims=True))
            l = jnp.sum(e, axis=-1, keepdims=True)
            o = jnp.dot(e.astype(BF16), vh, preferred_element_type=F32) * (1.0 / l)
            y_ref[:, (Y_XATT + hd) * LANES:(Y_XATT + hd + 1) * LANES] = (
                o * _half_silu(h_cur[T_XATT_GATE + hd])).astype(BF16)

        vec += [(60, functools.partial(attend, hd)) for hd in range(N_XHEADS)]

        mxu = [functools.partial(project_cols, base + SUB, h_nxt, n0, PROJ_TILES)
               for n0 in range(0, N_IN_T, PROJ_TILES)]
        total = sum(c for c, _ in vec)
        done, issued = 0, 0
        for cost, task in vec:
            while issued < len(mxu) and issued * total <= done * len(mxu):
                mxu[issued]()
                issued += 1
            task()
            done += cost
        for task in mxu[issued:]:
            task()

        z = jnp.dot(y_ref[...], w_out_ref[...], preferred_element_type=F32)
        for rb in range(SUB // SUBLANES):
            rows = pl.ds(base + rb * SUBLANES, SUBLANES)
            zt = [z[rb * SUBLANES:(rb + 1) * SUBLANES, n * LANES:(n + 1) * LANES]
                  + lane_tile(b_out_ref, n) + alpha * x_ref[0, rows, n * LANES:(n + 1) * LANES]
                  for n in range(N_OUT_T)]
            cen, inv = _row_stats(zt, D_MODEL)
            for n in range(N_OUT_T):
                o_ref[0, rows, n * LANES:(n + 1) * LANES] = (
                    cen[n] * inv * lane_tile(lg_ref, n) + lane_tile(lb_ref, n))

    def step(s, carry):
        slot = s % 2
        sub_block(s, h_ref.at[slot], h_ref.at[1 - slot])
        return carry

    lax.fori_loop(0, N_SUB, step, 0)


def _const_spec(shape):
    return pl.BlockSpec(shape, lambda *_: (0,) * len(shape), pipeline_mode=pl.Buffered(1))


def _layer(x, mem, w_in, b_in, conv_w, conv_b, conv_ln_g, conv_ln_b, gmlp_ln_g, gmlp_ln_b,
           gmlp_ws, gmlp_bs, w_kv, b_kv, w_out, b_out, ln_g, ln_b, alpha):
    batch, seq, d = x.shape
    assert d == D_MODEL and seq % SEQ_TILE == 0 and N_SUB % 2 == 0
    assert mem.shape == (batch, MEM_LEN, D_MODEL)
    n_seq_tiles = seq // SEQ_TILE
    row = lambda a: a.reshape(1, -1)

    kt, v = pl.pallas_call(
        _kv_kernel,
        grid=(batch,),
        in_specs=[pl.BlockSpec((1, MEM_LEN, D_MODEL), lambda b: (b, 0, 0)),
                  _const_spec((D_MODEL, 2 * D_XATT)),
                  _const_spec((1, 2 * D_XATT))],
        out_specs=[pl.BlockSpec((1, D_XATT, MEM_LEN), lambda b: (b, 0, 0)),
                   pl.BlockSpec((1, MEM_LEN, D_XATT), lambda b: (b, 0, 0))],
        out_shape=[jax.ShapeDtypeStruct((batch, D_XATT, MEM_LEN), BF16),
                   jax.ShapeDtypeStruct((batch, MEM_LEN, D_XATT), BF16)],
        name="mem_kv",
    )(mem, w_kv.astype(BF16), row(b_kv))

    half = jnp.full((D_IN_PROJ,), 0.5, F32).at[T_XATT_Q * LANES:T_XATT_GATE * LANES].set(1.0)
    w_in_s = (w_in * half).astype(BF16)
    b_in_s = row(b_in * half)
    bs_cols = jnp.broadcast_to(gmlp_bs[:, :, None], (N_GMLP_HEADS, CHUNK, LANES))

    def next_sub_block(b, j):
        nxt = jnp.minimum(b * n_seq_tiles + j + 1, batch * n_seq_tiles - 1)
        return nxt // n_seq_tiles, (nxt % n_seq_tiles) * N_SUB, 0

    return pl.pallas_call(
        functools.partial(_layer_kernel, alpha=alpha),
        grid=(batch, n_seq_tiles),
        in_specs=[pl.BlockSpec((1, SEQ_TILE, D_MODEL), lambda b, j: (b, j, 0)),
                  pl.BlockSpec((1, SUB, D_MODEL), next_sub_block),
                  pl.BlockSpec((1, D_XATT, MEM_LEN), lambda b, j: (b, 0, 0)),
                  pl.BlockSpec((1, MEM_LEN, D_XATT), lambda b, j: (b, 0, 0)),
                  _const_spec((D_MODEL, D_IN_PROJ)),
                  _const_spec((1, D_IN_PROJ)),
                  _const_spec((CONV_WIDTH, D_CONV)),
                  _const_spec((1, D_CONV)), _const_spec((1, D_CONV)), _const_spec((1, D_CONV)),
                  _const_spec((1, D_GMLP)), _const_spec((1, D_GMLP)),
                  _const_spec((N_GMLP_HEADS, CHUNK, CHUNK)),
                  _const_spec((N_GMLP_HEADS, CHUNK, LANES)),
                  _const_spec((D_MIX, D_MODEL)),
                  _const_spec((1, D_MODEL)), _const_spec((1, D_MODEL)), _const_spec((1, D_MODEL))],
        out_specs=pl.BlockSpec((1, SEQ_TILE, D_MODEL), lambda b, j: (b, j, 0)),
        out_shape=jax.ShapeDtypeStruct((batch, seq, D_MODEL), F32),
        scratch_shapes=[
            pltpu.VMEM((SEQ_TILE + SUB, D_MODEL), BF16),
            pltpu.VMEM((2, N_IN_T, SUB, LANES), F32),
            pltpu.VMEM((N_CONV_T, SEQ_TILE + CONV_HIST, LANES), F32),
            pltpu.VMEM((N_CONV_T, SUB, LANES), F32),
            pltpu.VMEM((N_GMLP_HEADS, SUB, LANES), BF16),
            pltpu.VMEM((N_GMLP_HEADS, SUB, LANES), F32),
            pltpu.VMEM((SUB, D_MIX), BF16),
            pltpu.VMEM((N_GMLP_HEADS, CHUNK, CHUNK), BF16),
        ],
        compiler_params=pltpu.CompilerParams(
            dimension_semantics=("arbitrary", "arbitrary"),
            vmem_limit_bytes=V7X_VMEM_BYTES - 8 * 1024 * 1024),
        name="hybrid_layer",
    )(x, x, kt, v, w_in_s, b_in_s, conv_w, row(conv_b), row(conv_ln_g),
      row(conv_ln_b), row(gmlp_ln_g), row(gmlp_ln_b), gmlp_ws, bs_cols, w_out.astype(BF16),
      row(b_out), row(ln_g), row(ln_b))


def kernel(x, mem, w_in, b_in, conv_w, conv_b, conv_ln_g, conv_ln_b, gmlp_ln_g, gmlp_ln_b,
           gmlp_ws, gmlp_bs, w_kv, b_kv, w_out, b_out, ln_g, ln_b):
    depth = w_in.shape[0]
    alpha = (2.0 * depth) ** 0.25
    for l in range(depth):
        x = _layer(x, mem, w_in[l], b_in[l], conv_w[l], conv_b[l], conv_ln_g[l], conv_ln_b[l],
                   gmlp_ln_g[l], gmlp_ln_b[l], gmlp_ws[l], gmlp_bs[l], w_kv[l], b_kv[l],
                   w_out[l], b_out[l], ln_g[l], ln_b[l], alpha)
    return x
```

```python
import functools
import math

import jax
import jax.numpy as jnp
from jax import lax
from jax.experimental import pallas as pl
from jax.experimental.pallas import tpu as pltpu

F32 = jnp.float32
BF16 = jnp.bfloat16

D_MODEL = 1024
MEM_LEN = 256
D_MIX = 2 * D_MODEL
D_CONV = 3 * D_MIX // 8
D_GMLP = 3 * D_MIX // 8
D_XATT = D_MIX - D_CONV - D_GMLP
N_XHEADS = 4
XHEAD_DIM = D_XATT // N_XHEADS
CONV_WIDTH = 31
CHUNK = 128
N_GMLP_HEADS = D_GMLP // CHUNK
LN_EPS = 1e-5
D_IN_PROJ = 3 * D_CONV + 3 * D_GMLP + 2 * D_XATT

LANES = 128
SUBLANES = 8
PACKED_ROWS = 16
V7X_VMEM_BYTES = 64 * 1024 * 1024

SEQ_TILE = 512
SUB = 4 * CHUNK
N_SUB = SEQ_TILE // SUB
OUT_ROWS = 256
CONV_HIST = 32
CONV_ROW_STRIDE = 4
CONV_BLOCK = SUBLANES * CONV_ROW_STRIDE

N_CONV_T = D_CONV // LANES
N_GMLP_T = D_GMLP // LANES
N_XATT_T = D_XATT // LANES
N_IN_T = D_IN_PROJ // LANES
N_OUT_T = D_MODEL // LANES
T_CONV_A, T_CONV_GLU, T_CONV_GATE = 0, N_CONV_T, 2 * N_CONV_T
T_GMLP_U, T_GMLP_V, T_GMLP_GATE = 3 * N_CONV_T, 3 * N_CONV_T + N_GMLP_T, 3 * N_CONV_T + 2 * N_GMLP_T
T_XATT_Q = 3 * N_CONV_T + 3 * N_GMLP_T
T_XATT_GATE = T_XATT_Q + N_XATT_T
Y_CONV, Y_GMLP, Y_XATT = 0, N_CONV_T, N_CONV_T + N_GMLP_T

GELU_C = math.sqrt(2.0 / math.pi)


def _half_silu(hx):
    return hx + hx * jnp.tanh(hx)


def _half_gelu_tanh(hx):
    inner = hx * (2.0 * GELU_C + (8.0 * GELU_C * 0.044715) * (hx * hx))
    return hx + hx * jnp.tanh(inner)


def _row_stats(tiles, width):
    tot = tiles[0]
    for t in tiles[1:]:
        tot = tot + t
    mu = jnp.sum(tot, axis=-1, keepdims=True) * (1.0 / width)
    cen = [t - mu for t in tiles]
    sq = cen[0] * cen[0]
    for c in cen[1:]:
        sq = sq + c * c
    var = jnp.sum(sq, axis=-1, keepdims=True) * (1.0 / width)
    return cen, lax.rsqrt(var + LN_EPS)


def _kv_kernel(mem_ref, w_ref, b_ref, kt_ref, v_ref):
    kv = jnp.dot(mem_ref[0].astype(BF16), w_ref[...], preferred_element_type=F32) + b_ref[...]
    kt_ref[0] = kv[:, :D_XATT].T.astype(BF16)
    v_ref[0] = kv[:, D_XATT:].astype(BF16)


def _layer_kernel(x_ref, kt_ref, v_ref, w_in_ref, b_in_ref, cw_ref, cb_ref, clg_ref,
                  clb_ref, glg_ref, glb_ref, ws_ref, bs_ref, w_out_ref, b_out_ref, lg_ref, lb_ref,
                  o_ref, xb_ref, cbuf_ref, cout_ref, sg_ref, vg_ref, vn_ref, ug_ref, y_ref,
                  wsm_ref, *, alpha):
    ts = SEQ_TILE

    def lane_tile(ref, n):
        return ref[0:1, n * LANES:(n + 1) * LANES]

    xb_ref[...] = x_ref[0].astype(BF16)

    @pl.when(pl.program_id(1) == 0)
    def _():
        cbuf_ref[:, 0:CONV_HIST, :] = jnp.zeros((N_CONV_T, CONV_HIST, LANES), F32)

    @pl.when(pl.program_id(1) > 0)
    def _():
        cbuf_ref[:, 0:CONV_HIST, :] = cbuf_ref[:, ts:ts + CONV_HIST, :]

    tril = (lax.broadcasted_iota(jnp.int32, (CHUNK, CHUNK), 0)
            >= lax.broadcasted_iota(jnp.int32, (CHUNK, CHUNK), 1))
    for hd in range(N_GMLP_HEADS):
        wsm_ref[hd] = jnp.where(tril, ws_ref[hd], 0.0).astype(BF16)
    att_scale = (XHEAD_DIM ** -0.5) * math.log2(math.e)

    def sub_block(s, carry):
        base = pl.multiple_of(s * SUB, SUB)

        def project(t0, t1):
            w = jnp.concatenate([w_in_ref[:, t0 * LANES:(t0 + 1) * LANES],
                                 w_in_ref[:, t1 * LANES:(t1 + 1) * LANES]], axis=1)
            res = jnp.dot(xb_ref[pl.ds(base, SUB), :], w, preferred_element_type=F32)
            return (res[:, :LANES] + lane_tile(b_in_ref, t0),
                    res[:, LANES:] + lane_tile(b_in_ref, t1))

        for cg in range(N_CONV_T):
            a, g = project(T_CONV_A + cg, T_CONV_GLU + cg)
            cbuf_ref[cg, pl.ds(CONV_HIST + base, SUB), :] = a + a * jnp.tanh(g)
        for cg in range(0, N_CONV_T, 2):
            g0, g1 = project(T_CONV_GATE + cg, T_CONV_GATE + cg + 1)
            sg_ref[cg] = _half_silu(g0)
            sg_ref[cg + 1] = _half_silu(g1)
        for rb in range(SUB // CONV_BLOCK):
            r = rb * CONV_BLOCK
            first = base + r + CONV_HIST - (CONV_WIDTH - 1)
            for cg in range(N_CONV_T):
                acc = [None] * CONV_ROW_STRIDE
                for k in range(CONV_WIDTH):
                    w = cw_ref[k:k + 1, cg * LANES:(cg + 1) * LANES]
                    for p in range(CONV_ROW_STRIDE):
                        rows = pl.ds(first + p + k, SUBLANES, stride=CONV_ROW_STRIDE)
                        term = w * cbuf_ref[cg, rows, :]
                        acc[p] = term if acc[p] is None else acc[p] + term
                for p in range(CONV_ROW_STRIDE):
                    cout_ref[cg, pl.ds(r + p, SUBLANES, stride=CONV_ROW_STRIDE), :] = acc[p]
            for i in range(CONV_BLOCK // PACKED_ROWS):
                rows = pl.ds(r + i * PACKED_ROWS, PACKED_ROWS)
                cv = [cout_ref[cg, rows, :] + lane_tile(cb_ref, cg) for cg in range(N_CONV_T)]
                cen, inv = _row_stats(cv, D_CONV)
                for cg in range(N_CONV_T):
                    hn = cen[cg] * inv * lane_tile(clg_ref, cg) + lane_tile(clb_ref, cg)
                    y_ref[rows, (Y_CONV + cg) * LANES:(Y_CONV + cg + 1) * LANES] = (
                        _half_silu(0.5 * hn) * sg_ref[cg, rows, :]).astype(BF16)

        for hd in range(0, N_GMLP_HEADS, 2):
            v0, v1 = project(T_GMLP_V + hd, T_GMLP_V + hd + 1)
            vg_ref[hd] = _half_gelu_tanh(v0)
            vg_ref[hd + 1] = _half_gelu_tanh(v1)
        for hd in range(N_GMLP_HEADS):
            u, g = project(T_GMLP_U + hd, T_GMLP_GATE + hd)
            ug_ref[hd] = _half_gelu_tanh(u) * _half_silu(g)
        for rb in range(SUB // PACKED_ROWS):
            rows = pl.ds(rb * PACKED_ROWS, PACKED_ROWS)
            cen, inv = _row_stats([vg_ref[hd, rows, :] for hd in range(N_GMLP_HEADS)], D_GMLP)
            for hd in range(N_GMLP_HEADS):
                vn = cen[hd] * inv * lane_tile(glg_ref, hd) + lane_tile(glb_ref, hd)
                vn_ref[hd, rows, :] = vn.astype(BF16)
        for hd in range(N_GMLP_HEADS):
            for c in range(SUB // CHUNK):
                rows = pl.ds(c * CHUNK, CHUNK)
                mix = jnp.dot(wsm_ref[hd], vn_ref[hd, rows, :],
                              preferred_element_type=F32) + bs_ref[hd]
                y_ref[rows, (Y_GMLP + hd) * LANES:(Y_GMLP + hd + 1) * LANES] = (
                    ug_ref[hd, rows, :] * mix).astype(BF16)

        for hd in range(N_XHEADS):
            q, g = project(T_XATT_Q + hd, T_XATT_GATE + hd)
            kt = kt_ref[0, hd * XHEAD_DIM:(hd + 1) * XHEAD_DIM, :]
            vh = v_ref[0, :, hd * XHEAD_DIM:(hd + 1) * XHEAD_DIM]
            t = jnp.dot(q.astype(BF16), kt, preferred_element_type=F32) * att_scale
            e = jnp.exp2(t - jnp.max(t, axis=-1, keepdims=True))
            l = jnp.sum(e, axis=-1, keepdims=True)
            o = jnp.dot(e.astype(BF16), vh, preferred_element_type=F32) * (1.0 / l)
            y_ref[:, (Y_XATT + hd) * LANES:(Y_XATT + hd + 1) * LANES] = (
                o * _half_silu(g)).astype(BF16)

        for piece in range(SUB // OUT_ROWS):
            z = jnp.dot(y_ref[piece * OUT_ROWS:(piece + 1) * OUT_ROWS, :], w_out_ref[...],
                        preferred_element_type=F32)
            for rb in range(OUT_ROWS // SUBLANES):
                rows = pl.ds(base + piece * OUT_ROWS + rb * SUBLANES, SUBLANES)
                zt = [z[rb * SUBLANES:(rb + 1) * SUBLANES, n * LANES:(n + 1) * LANES]
                      + lane_tile(b_out_ref, n) + alpha * x_ref[0, rows, n * LANES:(n + 1) * LANES]
                      for n in range(N_OUT_T)]
                cen, inv = _row_stats(zt, D_MODEL)
                for n in range(N_OUT_T):
                    o_ref[0, rows, n * LANES:(n + 1) * LANES] = (
                        cen[n] * inv * lane_tile(lg_ref, n) + lane_tile(lb_ref, n))
        return carry

    lax.fori_loop(0, N_SUB, sub_block, 0)


def _const_spec(shape):
    return pl.BlockSpec(shape, lambda *_: (0,) * len(shape), pipeline_mode=pl.Buffered(1))


def _layer(x, mem, w_in, b_in, conv_w, conv_b, conv_ln_g, conv_ln_b, gmlp_ln_g, gmlp_ln_b,
           gmlp_ws, gmlp_bs, w_kv, b_kv, w_out, b_out, ln_g, ln_b, alpha):
    batch, seq, d = x.shape
    assert d == D_MODEL and seq % SEQ_TILE == 0
    assert mem.shape == (batch, MEM_LEN, D_MODEL)
    n_seq_tiles = seq // SEQ_TILE
    row = lambda a: a.reshape(1, -1)

    kt, v = pl.pallas_call(
        _kv_kernel,
        grid=(batch,),
        in_specs=[pl.BlockSpec((1, MEM_LEN, D_MODEL), lambda b: (b, 0, 0)),
                  _const_spec((D_MODEL, 2 * D_XATT)),
                  _const_spec((1, 2 * D_XATT))],
        out_specs=[pl.BlockSpec((1, D_XATT, MEM_LEN), lambda b: (b, 0, 0)),
                   pl.BlockSpec((1, MEM_LEN, D_XATT), lambda b: (b, 0, 0))],
        out_shape=[jax.ShapeDtypeStruct((batch, D_XATT, MEM_LEN), BF16),
                   jax.ShapeDtypeStruct((batch, MEM_LEN, D_XATT), BF16)],
        name="mem_kv",
    )(mem, w_kv.astype(BF16), row(b_kv))

    half = jnp.full((D_IN_PROJ,), 0.5, F32).at[T_XATT_Q * LANES:T_XATT_GATE * LANES].set(1.0)
    w_in_s = (w_in * half).astype(BF16)
    b_in_s = row(b_in * half)
    bs_cols = jnp.broadcast_to(gmlp_bs[:, :, None], (N_GMLP_HEADS, CHUNK, LANES))

    return pl.pallas_call(
        functools.partial(_layer_kernel, alpha=alpha),
        grid=(batch, n_seq_tiles),
        in_specs=[pl.BlockSpec((1, SEQ_TILE, D_MODEL), lambda b, j: (b, j, 0)),
                  pl.BlockSpec((1, D_XATT, MEM_LEN), lambda b, j: (b, 0, 0)),
                  pl.BlockSpec((1, MEM_LEN, D_XATT), lambda b, j: (b, 0, 0)),
                  _const_spec((D_MODEL, D_IN_PROJ)),
                  _const_spec((1, D_IN_PROJ)),
                  _const_spec((CONV_WIDTH, D_CONV)),
                  _const_spec((1, D_CONV)), _const_spec((1, D_CONV)), _const_spec((1, D_CONV)),
                  _const_spec((1, D_GMLP)), _const_spec((1, D_GMLP)),
                  _const_spec((N_GMLP_HEADS, CHUNK, CHUNK)),
                  _const_spec((N_GMLP_HEADS, CHUNK, LANES)),
                  _const_spec((D_MIX, D_MODEL)),
                  _const_spec((1, D_MODEL)), _const_spec((1, D_MODEL)), _const_spec((1, D_MODEL))],
        out_specs=pl.BlockSpec((1, SEQ_TILE, D_MODEL), lambda b, j: (b, j, 0)),
        out_shape=jax.ShapeDtypeStruct((batch, seq, D_MODEL), F32),
        scratch_shapes=[
            pltpu.VMEM((SEQ_TILE, D_MODEL), BF16),
            pltpu.VMEM((N_CONV_T, SEQ_TILE + CONV_HIST, LANES), F32),
            pltpu.VMEM((N_CONV_T, SUB, LANES), F32),
            pltpu.VMEM((N_CONV_T, SUB, LANES), F32),
            pltpu.VMEM((N_GMLP_HEADS, SUB, LANES), F32),
            pltpu.VMEM((N_GMLP_HEADS, SUB, LANES), BF16),
            pltpu.VMEM((N_GMLP_HEADS, SUB, LANES), F32),
            pltpu.VMEM((SUB, D_MIX), BF16),
            pltpu.VMEM((N_GMLP_HEADS, CHUNK, CHUNK), BF16),
        ],
        compiler_params=pltpu.CompilerParams(
            dimension_semantics=("arbitrary", "arbitrary"),
            vmem_limit_bytes=V7X_VMEM_BYTES - 8 * 1024 * 1024),
        name="hybrid_layer",
    )(x, kt, v, w_in_s, b_in_s, conv_w, row(conv_b), row(conv_ln_g),
      row(conv_ln_b), row(gmlp_ln_g), row(gmlp_ln_b), gmlp_ws, bs_cols, w_out.astype(BF16),
      row(b_out), row(ln_g), row(ln_b))


def kernel(x, mem, w_in, b_in, conv_w, conv_b, conv_ln_g, conv_ln_b, gmlp_ln_g, gmlp_ln_b,
           gmlp_ws, gmlp_bs, w_kv, b_kv, w_out, b_out, ln_g, ln_b):
    depth = w_in.shape[0]
    alpha = (2.0 * depth) ** 0.25
    for l in range(depth):
        x = _layer(x, mem, w_in[l], b_in[l], conv_w[l], conv_b[l], conv_ln_g[l], conv_ln_b[l],
                   gmlp_ln_g[l], gmlp_ln_b[l], gmlp_ws[l], gmlp_bs[l], w_kv[l], b_kv[l],
                   w_out[l], b_out[l], ln_g[l], ln_b[l], alpha)
    return x
```

```python
import functools
import math

import jax
import jax.numpy as jnp
from jax import lax
from jax.experimental import pallas as pl
from jax.experimental.pallas import tpu as pltpu

F32 = jnp.float32
BF16 = jnp.bfloat16

D_MODEL = 1024
MEM_LEN = 256
D_MIX = 2 * D_MODEL
D_CONV = 3 * D_MIX // 8
D_GMLP = 3 * D_MIX // 8
D_XATT = D_MIX - D_CONV - D_GMLP
N_XHEADS = 4
XHEAD_DIM = D_XATT // N_XHEADS
CONV_WIDTH = 31
CHUNK = 128
N_GMLP_HEADS = D_GMLP // CHUNK
LN_EPS = 1e-5
D_IN_PROJ = 3 * D_CONV + 3 * D_GMLP + 2 * D_XATT

LANES = 128
SUBLANES = 8
PACKED_ROWS = 16
V7X_VMEM_BYTES = 64 * 1024 * 1024
VMEM_RESERVE_BYTES = 2 * 1024 * 1024

SEQ_TILE = 1024
SUB = 8 * CHUNK
N_SUB = SEQ_TILE // SUB
OUT_ROWS = 256
CONV_HIST = 32
CONV_ROW_STRIDE = 4
CONV_BLOCK = SUBLANES * CONV_ROW_STRIDE

N_CONV_T = D_CONV // LANES
N_GMLP_T = D_GMLP // LANES
N_XATT_T = D_XATT // LANES
N_IN_T = D_IN_PROJ // LANES
N_OUT_T = D_MODEL // LANES
T_CONV_A, T_CONV_GLU, T_CONV_GATE = 0, N_CONV_T, 2 * N_CONV_T
T_GMLP_U, T_GMLP_V, T_GMLP_GATE = 3 * N_CONV_T, 3 * N_CONV_T + N_GMLP_T, 3 * N_CONV_T + 2 * N_GMLP_T
T_XATT_Q = 3 * N_CONV_T + 3 * N_GMLP_T
T_XATT_GATE = T_XATT_Q + N_XATT_T
Y_CONV, Y_GMLP, Y_XATT = 0, N_CONV_T, N_CONV_T + N_GMLP_T

GELU_C = math.sqrt(2.0 / math.pi)


def _half_silu(hx):
    return hx + hx * jnp.tanh(hx)


def _half_gelu_tanh(hx):
    inner = hx * (2.0 * GELU_C + (8.0 * GELU_C * 0.044715) * (hx * hx))
    return hx + hx * jnp.tanh(inner)


def _row_stats(tiles, width):
    tot = tiles[0]
    for t in tiles[1:]:
        tot = tot + t
    mu = jnp.sum(tot, axis=-1, keepdims=True) * (1.0 / width)
    cen = [t - mu for t in tiles]
    sq = cen[0] * cen[0]
    for c in cen[1:]:
        sq = sq + c * c
    var = jnp.sum(sq, axis=-1, keepdims=True) * (1.0 / width)
    return cen, lax.rsqrt(var + LN_EPS)


def _kv_kernel(mem_ref, w_ref, b_ref, kt_ref, v_ref):
    kv = jnp.dot(mem_ref[0].astype(BF16), w_ref[...], preferred_element_type=F32) + b_ref[...]
    kt_ref[0] = kv[:, :D_XATT].T.astype(BF16)
    v_ref[0] = kv[:, D_XATT:].astype(BF16)


def _layer_kernel(x_ref, kt_ref, v_ref, w_in_ref, b_in_ref, cw_ref, cb_ref, clg_ref,
                  clb_ref, glg_ref, glb_ref, ws_ref, bs_ref, w_out_ref, b_out_ref, lg_ref, lb_ref,
                  o_ref, xb_ref, cbuf_ref, cout_ref, sg_ref, vg_ref, vn_ref, ug_ref, y_ref,
                  wsm_ref, *, alpha):
    ts = SEQ_TILE

    def lane_tile(ref, n):
        return ref[0:1, n * LANES:(n + 1) * LANES]

    xb_ref[...] = x_ref[0].astype(BF16)

    @pl.when(pl.program_id(1) == 0)
    def _():
        cbuf_ref[:, 0:CONV_HIST, :] = jnp.zeros((N_CONV_T, CONV_HIST, LANES), F32)

    @pl.when(pl.program_id(1) > 0)
    def _():
        cbuf_ref[:, 0:CONV_HIST, :] = cbuf_ref[:, ts:ts + CONV_HIST, :]

    tril = (lax.broadcasted_iota(jnp.int32, (CHUNK, CHUNK), 0)
            >= lax.broadcasted_iota(jnp.int32, (CHUNK, CHUNK), 1))
    for hd in range(N_GMLP_HEADS):
        wsm_ref[hd] = jnp.where(tril, ws_ref[hd], 0.0).astype(BF16)
    att_scale = (XHEAD_DIM ** -0.5) * math.log2(math.e)

    def sub_block(s, carry):
        base = pl.multiple_of(s * SUB, SUB)

        def project(t0, t1):
            w = jnp.concatenate([w_in_ref[:, t0 * LANES:(t0 + 1) * LANES],
                                 w_in_ref[:, t1 * LANES:(t1 + 1) * LANES]], axis=1)
            res = jnp.dot(xb_ref[pl.ds(base, SUB), :], w, preferred_element_type=F32)
            return (res[:, :LANES] + lane_tile(b_in_ref, t0),
                    res[:, LANES:] + lane_tile(b_in_ref, t1))

        for cg in range(N_CONV_T):
            a, g = project(T_CONV_A + cg, T_CONV_GLU + cg)
            cbuf_ref[cg, pl.ds(CONV_HIST + base, SUB), :] = a + a * jnp.tanh(g)
        for cg in range(0, N_CONV_T, 2):
            g0, g1 = project(T_CONV_GATE + cg, T_CONV_GATE + cg + 1)
            sg_ref[cg] = _half_silu(g0)
            sg_ref[cg + 1] = _half_silu(g1)
        for rb in range(SUB // CONV_BLOCK):
            r = rb * CONV_BLOCK
            first = base + r + CONV_HIST - (CONV_WIDTH - 1)
            for cg in range(N_CONV_T):
                acc = [None] * CONV_ROW_STRIDE
                for k in range(CONV_WIDTH):
                    w = cw_ref[k:k + 1, cg * LANES:(cg + 1) * LANES]
                    for p in range(CONV_ROW_STRIDE):
                        rows = pl.ds(first + p + k, SUBLANES, stride=CONV_ROW_STRIDE)
                        term = w * cbuf_ref[cg, rows, :]
                        acc[p] = term if acc[p] is None else acc[p] + term
                for p in range(CONV_ROW_STRIDE):
                    cout_ref[cg, pl.ds(r + p, SUBLANES, stride=CONV_ROW_STRIDE), :] = acc[p]
            for i in range(CONV_BLOCK // PACKED_ROWS):
                rows = pl.ds(r + i * PACKED_ROWS, PACKED_ROWS)
                cv = [cout_ref[cg, rows, :] + lane_tile(cb_ref, cg) for cg in range(N_CONV_T)]
                cen, inv = _row_stats(cv, D_CONV)
                for cg in range(N_CONV_T):
                    hn = cen[cg] * inv * lane_tile(clg_ref, cg) + lane_tile(clb_ref, cg)
                    y_ref[rows, (Y_CONV + cg) * LANES:(Y_CONV + cg + 1) * LANES] = (
                        _half_silu(0.5 * hn) * sg_ref[cg, rows, :]).astype(BF16)

        for hd in range(0, N_GMLP_HEADS, 2):
            v0, v1 = project(T_GMLP_V + hd, T_GMLP_V + hd + 1)
            vg_ref[hd] = _half_gelu_tanh(v0)
            vg_ref[hd + 1] = _half_gelu_tanh(v1)
        for hd in range(N_GMLP_HEADS):
            u, g = project(T_GMLP_U + hd, T_GMLP_GATE + hd)
            ug_ref[hd] = _half_gelu_tanh(u) * _half_silu(g)
        for rb in range(SUB // PACKED_ROWS):
            rows = pl.ds(rb * PACKED_ROWS, PACKED_ROWS)
            cen, inv = _row_stats([vg_ref[hd, rows, :] for hd in range(N_GMLP_HEADS)], D_GMLP)
            for hd in range(N_GMLP_HEADS):
                vn = cen[hd] * inv * lane_tile(glg_ref, hd) + lane_tile(glb_ref, hd)
                vn_ref[hd, rows, :] = vn.astype(BF16)
        for hd in range(N_GMLP_HEADS):
            for c in range(SUB // CHUNK):
                rows = pl.ds(c * CHUNK, CHUNK)
                mix = jnp.dot(wsm_ref[hd], vn_ref[hd, rows, :],
                              preferred_element_type=F32) + bs_ref[hd]
                y_ref[rows, (Y_GMLP + hd) * LANES:(Y_GMLP + hd + 1) * LANES] = (
                    ug_ref[hd, rows, :] * mix).astype(BF16)

        for hd in range(N_XHEADS):
            q, g = project(T_XATT_Q + hd, T_XATT_GATE + hd)
            kt = kt_ref[0, hd * XHEAD_DIM:(hd + 1) * XHEAD_DIM, :]
            vh = v_ref[0, :, hd * XHEAD_DIM:(hd + 1) * XHEAD_DIM]
            t = jnp.dot(q.astype(BF16), kt, preferred_element_type=F32) * att_scale
            e = jnp.exp2(t - jnp.max(t, axis=-1, keepdims=True))
            l = jnp.sum(e, axis=-1, keepdims=True)
            o = jnp.dot(e.astype(BF16), vh, preferred_element_type=F32) * (1.0 / l)
            y_ref[:, (Y_XATT + hd) * LANES:(Y_XATT + hd + 1) * LANES] = (
                o * _half_silu(g)).astype(BF16)

        for piece in range(SUB // OUT_ROWS):
            z = jnp.dot(y_ref[piece * OUT_ROWS:(piece + 1) * OUT_ROWS, :], w_out_ref[...],
                        preferred_element_type=F32)
            for rb in range(OUT_ROWS // SUBLANES):
                rows = pl.ds(base + piece * OUT_ROWS + rb * SUBLANES, SUBLANES)
                zt = [z[rb * SUBLANES:(rb + 1) * SUBLANES, n * LANES:(n + 1) * LANES]
                      + lane_tile(b_out_ref, n) + alpha * x_ref[0, rows, n * LANES:(n + 1) * LANES]
                      for n in range(N_OUT_T)]
                cen, inv = _row_stats(zt, D_MODEL)
                for n in range(N_OUT_T):
                    o_ref[0, rows, n * LANES:(n + 1) * LANES] = (
                        cen[n] * inv * lane_tile(lg_ref, n) + lane_tile(lb_ref, n))
        return carry

    lax.fori_loop(0, N_SUB, sub_block, 0)


def _const_spec(shape):
    return pl.BlockSpec(shape, lambda *_: (0,) * len(shape), pipeline_mode=pl.Buffered(1))


def _layer(x, mem, w_in, b_in, conv_w, conv_b, conv_ln_g, conv_ln_b, gmlp_ln_g, gmlp_ln_b,
           gmlp_ws, gmlp_bs, w_kv, b_kv, w_out, b_out, ln_g, ln_b, alpha):
    batch, seq, d = x.shape
    assert d == D_MODEL and seq % SEQ_TILE == 0
    assert mem.shape == (batch, MEM_LEN, D_MODEL)
    n_seq_tiles = seq // SEQ_TILE
    row = lambda a: a.reshape(1, -1)

    kt, v = pl.pallas_call(
        _kv_kernel,
        grid=(batch,),
        in_specs=[pl.BlockSpec((1, MEM_LEN, D_MODEL), lambda b: (b, 0, 0)),
                  _const_spec((D_MODEL, 2 * D_XATT)),
                  _const_spec((1, 2 * D_XATT))],
        out_specs=[pl.BlockSpec((1, D_XATT, MEM_LEN), lambda b: (b, 0, 0)),
                   pl.BlockSpec((1, MEM_LEN, D_XATT), lambda b: (b, 0, 0))],
        out_shape=[jax.ShapeDtypeStruct((batch, D_XATT, MEM_LEN), BF16),
                   jax.ShapeDtypeStruct((batch, MEM_LEN, D_XATT), BF16)],
        name="mem_kv",
    )(mem, w_kv.astype(BF16), row(b_kv))

    half = jnp.full((D_IN_PROJ,), 0.5, F32).at[T_XATT_Q * LANES:T_XATT_GATE * LANES].set(1.0)
    w_in_s = (w_in * half).astype(BF16)
    b_in_s = row(b_in * half)
    bs_cols = jnp.broadcast_to(gmlp_bs[:, :, None], (N_GMLP_HEADS, CHUNK, LANES))

    return pl.pallas_call(
        functools.partial(_layer_kernel, alpha=alpha),
        grid=(batch, n_seq_tiles),
        in_specs=[pl.BlockSpec((1, SEQ_TILE, D_MODEL), lambda b, j: (b, j, 0)),
                  pl.BlockSpec((1, D_XATT, MEM_LEN), lambda b, j: (b, 0, 0)),
                  pl.BlockSpec((1, MEM_LEN, D_XATT), lambda b, j: (b, 0, 0)),
                  _const_spec((D_MODEL, D_IN_PROJ)),
                  _const_spec((1, D_IN_PROJ)),
                  _const_spec((CONV_WIDTH, D_CONV)),
                  _const_spec((1, D_CONV)), _const_spec((1, D_CONV)), _const_spec((1, D_CONV)),
                  _const_spec((1, D_GMLP)), _const_spec((1, D_GMLP)),
                  _const_spec((N_GMLP_HEADS, CHUNK, CHUNK)),
                  _const_spec((N_GMLP_HEADS, CHUNK, LANES)),
                  _const_spec((D_MIX, D_MODEL)),
                  _const_spec((1, D_MODEL)), _const_spec((1, D_MODEL)), _const_spec((1, D_MODEL))],
        out_specs=pl.BlockSpec((1, SEQ_TILE, D_MODEL), lambda b, j: (b, j, 0)),
        out_shape=jax.ShapeDtypeStruct((batch, seq, D_MODEL), F32),
        scratch_shapes=[
            pltpu.VMEM((SEQ_TILE, D_MODEL), BF16),
            pltpu.VMEM((N_CONV_T, SEQ_TILE + CONV_HIST, LANES), F32),
            pltpu.VMEM((N_CONV_T, SUB, LANES), F32),
            pltpu.VMEM((N_CONV_T, SUB, LANES), F32),
            pltpu.VMEM((N_GMLP_HEADS, SUB, LANES), F32),
            pltpu.VMEM((N_GMLP_HEADS, SUB, LANES), BF16),
            pltpu.VMEM((N_GMLP_HEADS, SUB, LANES), F32),
            pltpu.VMEM((SUB, D_MIX), BF16),
            pltpu.VMEM((N_GMLP_HEADS, CHUNK, CHUNK), BF16),
        ],
        compiler_params=pltpu.CompilerParams(
            dimension_semantics=("arbitrary", "arbitrary"),
            vmem_limit_bytes=V7X_VMEM_BYTES - VMEM_RESERVE_BYTES),
        name="hybrid_layer",
    )(x, kt, v, w_in_s, b_in_s, conv_w, row(conv_b), row(conv_ln_g),
      row(conv_ln_b), row(gmlp_ln_g), row(gmlp_ln_b), gmlp_ws, bs_cols, w_out.astype(BF16),
      row(b_out), row(ln_g), row(ln_b))


def kernel(x, mem, w_in, b_in, conv_w, conv_b, conv_ln_g, conv_ln_b, gmlp_ln_g, gmlp_ln_b,
           gmlp_ws, gmlp_bs, w_kv, b_kv, w_out, b_out, ln_g, ln_b):
    depth = w_in.shape[0]
    alpha = (2.0 * depth) ** 0.25
    for l in range(depth):
        x = _layer(x, mem, w_in[l], b_in[l], conv_w[l], conv_b[l], conv_ln_g[l], conv_ln_b[l],
                   gmlp_ln_g[l], gmlp_ln_b[l], gmlp_ws[l], gmlp_bs[l], w_kv[l], b_kv[l],
                   w_out[l], b_out[l], ln_g[l], ln_b[l], alpha)
    return x
```

```python
import functools
import math

import jax
import jax.numpy as jnp
from jax import lax
from jax.experimental import pallas as pl
from jax.experimental.pallas import tpu as pltpu

F32 = jnp.float32
BF16 = jnp.bfloat16

D_MODEL = 1024
MEM_LEN = 256
D_MIX = 2 * D_MODEL
D_CONV = 3 * D_MIX // 8
D_GMLP = 3 * D_MIX // 8
D_XATT = D_MIX - D_CONV - D_GMLP
N_XHEADS = 4
XHEAD_DIM = D_XATT // N_XHEADS
CONV_WIDTH = 31
CHUNK = 128
N_GMLP_HEADS = D_GMLP // CHUNK
LN_EPS = 1e-5
D_IN_PROJ = 3 * D_CONV + 3 * D_GMLP + 2 * D_XATT

LANES = 128
SUBLANES = 8
PACKED_ROWS = 16
V7X_VMEM_BYTES = 64 * 1024 * 1024
VMEM_RESERVE_BYTES = 2 * 1024 * 1024

SEQ_TILE = 1024
SUB = 8 * CHUNK
N_SUB = SEQ_TILE // SUB
OUT_ROWS = 256
KV_BATCH = 4
CONV_HIST = 32
CONV_ROW_STRIDE = 4
CONV_BLOCK = SUBLANES * CONV_ROW_STRIDE

N_CONV_T = D_CONV // LANES
N_GMLP_T = D_GMLP // LANES
N_XATT_T = D_XATT // LANES
N_IN_T = D_IN_PROJ // LANES
N_OUT_T = D_MODEL // LANES
T_CONV_A, T_CONV_GLU, T_CONV_GATE = 0, N_CONV_T, 2 * N_CONV_T
T_GMLP_U, T_GMLP_V, T_GMLP_GATE = 3 * N_CONV_T, 3 * N_CONV_T + N_GMLP_T, 3 * N_CONV_T + 2 * N_GMLP_T
T_XATT_Q = 3 * N_CONV_T + 3 * N_GMLP_T
T_XATT_GATE = T_XATT_Q + N_XATT_T
Y_CONV, Y_GMLP, Y_XATT = 0, N_CONV_T, N_CONV_T + N_GMLP_T

GELU_C = math.sqrt(2.0 / math.pi)


def _half_silu(hx):
    return hx + hx * jnp.tanh(hx)


def _half_gelu_tanh(hx):
    inner = hx * (2.0 * GELU_C + (8.0 * GELU_C * 0.044715) * (hx * hx))
    return hx + hx * jnp.tanh(inner)


def _row_stats(tiles, width):
    tot = tiles[0]
    for t in tiles[1:]:
        tot = tot + t
    mu = jnp.sum(tot, axis=-1, keepdims=True) * (1.0 / width)
    cen = [t - mu for t in tiles]
    sq = cen[0] * cen[0]
    for c in cen[1:]:
        sq = sq + c * c
    var = jnp.sum(sq, axis=-1, keepdims=True) * (1.0 / width)
    return cen, lax.rsqrt(var + LN_EPS)


def _kv_kernel(mem_ref, w_ref, b_ref, kt_ref, v_ref):
    nb = mem_ref.shape[0]
    mem = mem_ref[...].reshape(nb * MEM_LEN, D_MODEL).astype(BF16)
    kv = jnp.dot(mem, w_ref[...], preferred_element_type=F32) + b_ref[...]
    for i in range(nb):
        kv_i = kv[i * MEM_LEN:(i + 1) * MEM_LEN]
        kt_ref[i] = kv_i[:, :D_XATT].T.astype(BF16)
        v_ref[i] = kv_i[:, D_XATT:].astype(BF16)


def _layer_kernel(x_ref, kt_ref, v_ref, w_in_ref, b_in_ref, cw_ref, cb_ref, clg_ref,
                  clb_ref, glg_ref, glb_ref, ws_ref, bs_ref, w_out_ref, b_out_ref, lg_ref, lb_ref,
                  o_ref, xb_ref, cbuf_ref, cout_ref, sg_ref, vg_ref, vn_ref, ug_ref, y_ref,
                  wsm_ref, *, alpha):
    ts = SEQ_TILE

    def lane_tile(ref, n):
        return ref[0:1, n * LANES:(n + 1) * LANES]

    xb_ref[...] = x_ref[0].astype(BF16)

    @pl.when(pl.program_id(1) == 0)
    def _():
        cbuf_ref[:, 0:CONV_HIST, :] = jnp.zeros((N_CONV_T, CONV_HIST, LANES), F32)

    @pl.when(pl.program_id(1) > 0)
    def _():
        cbuf_ref[:, 0:CONV_HIST, :] = cbuf_ref[:, ts:ts + CONV_HIST, :]

    tril = (lax.broadcasted_iota(jnp.int32, (CHUNK, CHUNK), 0)
            >= lax.broadcasted_iota(jnp.int32, (CHUNK, CHUNK), 1))
    for hd in range(N_GMLP_HEADS):
        wsm_ref[hd] = jnp.where(tril, ws_ref[hd], 0.0).astype(BF16)
    att_scale = (XHEAD_DIM ** -0.5) * math.log2(math.e)

    def sub_block(s, carry):
        base = pl.multiple_of(s * SUB, SUB)

        def project(t0, t1):
            w = jnp.concatenate([w_in_ref[:, t0 * LANES:(t0 + 1) * LANES],
                                 w_in_ref[:, t1 * LANES:(t1 + 1) * LANES]], axis=1)
            res = jnp.dot(xb_ref[pl.ds(base, SUB), :], w, preferred_element_type=F32)
            return (res[:, :LANES] + lane_tile(b_in_ref, t0),
                    res[:, LANES:] + lane_tile(b_in_ref, t1))

        for cg in range(N_CONV_T):
            a, g = project(T_CONV_A + cg, T_CONV_GLU + cg)
            cbuf_ref[cg, pl.ds(CONV_HIST + base, SUB), :] = a + a * jnp.tanh(g)
        for cg in range(0, N_CONV_T, 2):
            g0, g1 = project(T_CONV_GATE + cg, T_CONV_GATE + cg + 1)
            sg_ref[cg] = _half_silu(g0)
            sg_ref[cg + 1] = _half_silu(g1)
        for rb in range(SUB // CONV_BLOCK):
            r = rb * CONV_BLOCK
            first = base + r + CONV_HIST - (CONV_WIDTH - 1)
            for cg in range(N_CONV_T):
                acc = [None] * CONV_ROW_STRIDE
                for k in range(CONV_WIDTH):
                    w = cw_ref[k:k + 1, cg * LANES:(cg + 1) * LANES]
                    for p in range(CONV_ROW_STRIDE):
                        rows = pl.ds(first + p + k, SUBLANES, stride=CONV_ROW_STRIDE)
                        term = w * cbuf_ref[cg, rows, :]
                        acc[p] = term if acc[p] is None else acc[p] + term
                for p in range(CONV_ROW_STRIDE):
                    cout_ref[cg, pl.ds(r + p, SUBLANES, stride=CONV_ROW_STRIDE), :] = acc[p]
            for i in range(CONV_BLOCK // PACKED_ROWS):
                rows = pl.ds(r + i * PACKED_ROWS, PACKED_ROWS)
                cv = [cout_ref[cg, rows, :] + lane_tile(cb_ref, cg) for cg in range(N_CONV_T)]
                cen, inv = _row_stats(cv, D_CONV)
                for cg in range(N_CONV_T):
                    hn = cen[cg] * inv * lane_tile(clg_ref, cg) + lane_tile(clb_ref, cg)
                    y_ref[rows, (Y_CONV + cg) * LANES:(Y_CONV + cg + 1) * LANES] = (
                        _half_silu(0.5 * hn) * sg_ref[cg, rows, :]).astype(BF16)

        for hd in range(0, N_GMLP_HEADS, 2):
            v0, v1 = project(T_GMLP_V + hd, T_GMLP_V + hd + 1)
            vg_ref[hd] = _half_gelu_tanh(v0)
            vg_ref[hd + 1] = _half_gelu_tanh(v1)
        for hd in range(N_GMLP_HEADS):
            u, g = project(T_GMLP_U + hd, T_GMLP_GATE + hd)
            ug_ref[hd] = _half_gelu_tanh(u) * _half_silu(g)
        for rb in range(SUB // PACKED_ROWS):
            rows = pl.ds(rb * PACKED_ROWS, PACKED_ROWS)
            cen, inv = _row_stats([vg_ref[hd, rows, :] for hd in range(N_GMLP_HEADS)], D_GMLP)
            for hd in range(N_GMLP_HEADS):
                vn = cen[hd] * inv * lane_tile(glg_ref, hd) + lane_tile(glb_ref, hd)
                vn_ref[hd, rows, :] = vn.astype(BF16)
        for hd in range(N_GMLP_HEADS):
            for c in range(SUB // CHUNK):
                rows = pl.ds(c * CHUNK, CHUNK)
                mix = jnp.dot(wsm_ref[hd], vn_ref[hd, rows, :],
                              preferred_element_type=F32) + bs_ref[hd]
                y_ref[rows, (Y_GMLP + hd) * LANES:(Y_GMLP + hd + 1) * LANES] = (
                    ug_ref[hd, rows, :] * mix).astype(BF16)

        for hd in range(N_XHEADS):
            q, g = project(T_XATT_Q + hd, T_XATT_GATE + hd)
            kt = kt_ref[0, hd * XHEAD_DIM:(hd + 1) * XHEAD_DIM, :]
            vh = v_ref[0, :, hd * XHEAD_DIM:(hd + 1) * XHEAD_DIM]
            t = jnp.dot(q.astype(BF16), kt, preferred_element_type=F32) * att_scale
            e = jnp.exp2(t - jnp.max(t, axis=-1, keepdims=True))
            l = jnp.sum(e, axis=-1, keepdims=True)
            o = jnp.dot(e.astype(BF16), vh, preferred_element_type=F32) * (1.0 / l)
            y_ref[:, (Y_XATT + hd) * LANES:(Y_XATT + hd + 1) * LANES] = (
                o * _half_silu(g)).astype(BF16)

        for piece in range(SUB // OUT_ROWS):
            z = jnp.dot(y_ref[piece * OUT_ROWS:(piece + 1) * OUT_ROWS, :], w_out_ref[...],
                        preferred_element_type=F32)
            for rb in range(OUT_ROWS // SUBLANES):
                rows = pl.ds(base + piece * OUT_ROWS + rb * SUBLANES, SUBLANES)
                zt = [z[rb * SUBLANES:(rb + 1) * SUBLANES, n * LANES:(n + 1) * LANES]
                      + lane_tile(b_out_ref, n) + alpha * x_ref[0, rows, n * LANES:(n + 1) * LANES]
                      for n in range(N_OUT_T)]
                cen, inv = _row_stats(zt, D_MODEL)
                for n in range(N_OUT_T):
                    o_ref[0, rows, n * LANES:(n + 1) * LANES] = (
                        cen[n] * inv * lane_tile(lg_ref, n) + lane_tile(lb_ref, n))
        return carry

    lax.fori_loop(0, N_SUB, sub_block, 0)


def _const_spec(shape):
    return pl.BlockSpec(shape, lambda *_: (0,) * len(shape), pipeline_mode=pl.Buffered(1))


def _layer(x, mem, w_in, b_in, conv_w, conv_b, conv_ln_g, conv_ln_b, gmlp_ln_g, gmlp_ln_b,
           gmlp_ws, gmlp_bs, w_kv, b_kv, w_out, b_out, ln_g, ln_b, alpha):
    batch, seq, d = x.shape
    assert d == D_MODEL and seq % SEQ_TILE == 0
    assert mem.shape == (batch, MEM_LEN, D_MODEL)
    n_seq_tiles = seq // SEQ_TILE
    kv_batch = math.gcd(batch, KV_BATCH)
    row = lambda a: a.reshape(1, -1)

    kt, v = pl.pallas_call(
        _kv_kernel,
        grid=(batch // kv_batch,),
        in_specs=[pl.BlockSpec((kv_batch, MEM_LEN, D_MODEL), lambda b: (b, 0, 0)),
                  _const_spec((D_MODEL, 2 * D_XATT)),
                  _const_spec((1, 2 * D_XATT))],
        out_specs=[pl.BlockSpec((kv_batch, D_XATT, MEM_LEN), lambda b: (b, 0, 0)),
                   pl.BlockSpec((kv_batch, MEM_LEN, D_XATT), lambda b: (b, 0, 0))],
        out_shape=[jax.ShapeDtypeStruct((batch, D_XATT, MEM_LEN), BF16),
                   jax.ShapeDtypeStruct((batch, MEM_LEN, D_XATT), BF16)],
        name="mem_kv",
    )(mem, w_kv.astype(BF16), row(b_kv))

    half = jnp.full((D_IN_PROJ,), 0.5, F32).at[T_XATT_Q * LANES:T_XATT_GATE * LANES].set(1.0)
    w_in_s = (w_in * half).astype(BF16)
    b_in_s = row(b_in * half)
    bs_cols = jnp.broadcast_to(gmlp_bs[:, :, None], (N_GMLP_HEADS, CHUNK, LANES))

    return pl.pallas_call(
        functools.partial(_layer_kernel, alpha=alpha),
        grid=(batch, n_seq_tiles),
        in_specs=[pl.BlockSpec((1, SEQ_TILE, D_MODEL), lambda b, j: (b, j, 0)),
                  pl.BlockSpec((1, D_XATT, MEM_LEN), lambda b, j: (b, 0, 0)),
                  pl.BlockSpec((1, MEM_LEN, D_XATT), lambda b, j: (b, 0, 0)),
                  _const_spec((D_MODEL, D_IN_PROJ)),
                  _const_spec((1, D_IN_PROJ)),
                  _const_spec((CONV_WIDTH, D_CONV)),
                  _const_spec((1, D_CONV)), _const_spec((1, D_CONV)), _const_spec((1, D_CONV)),
                  _const_spec((1, D_GMLP)), _const_spec((1, D_GMLP)),
                  _const_spec((N_GMLP_HEADS, CHUNK, CHUNK)),
                  _const_spec((N_GMLP_HEADS, CHUNK, LANES)),
                  _const_spec((D_MIX, D_MODEL)),
                  _const_spec((1, D_MODEL)), _const_spec((1, D_MODEL)), _const_spec((1, D_MODEL))],
        out_specs=pl.BlockSpec((1, SEQ_TILE, D_MODEL), lambda b, j: (b, j, 0)),
        out_shape=jax.ShapeDtypeStruct((batch, seq, D_MODEL), F32),
        scratch_shapes=[
            pltpu.VMEM((SEQ_TILE, D_MODEL), BF16),
            pltpu.VMEM((N_CONV_T, SEQ_TILE + CONV_HIST, LANES), F32),
            pltpu.VMEM((N_CONV_T, SUB, LANES), F32),
            pltpu.VMEM((N_CONV_T, SUB, LANES), F32),
            pltpu.VMEM((N_GMLP_HEADS, SUB, LANES), F32),
            pltpu.VMEM((N_GMLP_HEADS, SUB, LANES), BF16),
            pltpu.VMEM((N_GMLP_HEADS, SUB, LANES), F32),
            pltpu.VMEM((SUB, D_MIX), BF16),
            pltpu.VMEM((N_GMLP_HEADS, CHUNK, CHUNK), BF16),
        ],
        compiler_params=pltpu.CompilerParams(
            dimension_semantics=("arbitrary", "arbitrary"),
            vmem_limit_bytes=V7X_VMEM_BYTES - VMEM_RESERVE_BYTES),
        name="hybrid_layer",
    )(x, kt, v, w_in_s, b_in_s, conv_w, row(conv_b), row(conv_ln_g),
      row(conv_ln_b), row(gmlp_ln_g), row(gmlp_ln_b), gmlp_ws, bs_cols, w_out.astype(BF16),
      row(b_out), row(ln_g), row(ln_b))


def kernel(x, mem, w_in, b_in, conv_w, conv_b, conv_ln_g, conv_ln_b, gmlp_ln_g, gmlp_ln_b,
           gmlp_ws, gmlp_bs, w_kv, b_kv, w_out, b_out, ln_g, ln_b):
    depth = w_in.shape[0]
    alpha = (2.0 * depth) ** 0.25
    for l in range(depth):
        x = _layer(x, mem, w_in[l], b_in[l], conv_w[l], conv_b[l], conv_ln_g[l], conv_ln_b[l],
                   gmlp_ln_g[l], gmlp_ln_b[l], gmlp_ws[l], gmlp_bs[l], w_kv[l], b_kv[l],
                   w_out[l], b_out[l], ln_g[l], ln_b[l], alpha)
    return x
```
